```python
import math
import jax, jax.numpy as jnp
from jax import lax
import numpy as np

D_MODEL = 1024
BATCH = 2
SEQ = 8192
DEPTH = 1
DEC_BATCH = 128
DEC_SEQ = 1
PAST_LEN = 8192
PAGE_SIZE = 128

HEAD_DIM = 64
NSA_HEADS = D_MODEL // (2 * HEAD_DIM)
NSA_KV_GROUPS = max(1, NSA_HEADS // 4)
NSA_HPG = NSA_HEADS // NSA_KV_GROUPS
SB_HEADS = D_MODEL // (2 * HEAD_DIM)
CMP_BLOCK = 32
CMP_STRIDE = 16
SEL_BLOCK = 64
N_SEL = 16
WINDOW = 512
N_BUCKETS = 32
MAX_DISTANCE = 128
MEM_TOKENS = 256
MEM_HEADS = 4
D_FF = -(-(8 * D_MODEL) // (3 * 256)) * 256
Q_BLOCK = 128
EPS = 1e-6
NSA_WIDTH = NSA_HEADS * HEAD_DIM
SB_WIDTH = SB_HEADS * HEAD_DIM
KV_WIDTH = NSA_KV_GROUPS * HEAD_DIM
MEM_WIDTH = MEM_HEADS * HEAD_DIM
MIX_WIDTH = NSA_WIDTH + SB_WIDTH
IN_SIZES = (NSA_WIDTH, KV_WIDTH, KV_WIDTH, KV_WIDTH, KV_WIDTH, KV_WIDTH, KV_WIDTH, 3 * NSA_HEADS, SB_WIDTH, SB_WIDTH, SB_WIDTH)
IN_WIDTH = sum(IN_SIZES)

kernel_name = 'hybrid_nsa_stickbreak_memory_decode_step'


def rmsnorm(x, g):
    xf = x.astype(jnp.float32)
    xf = xf * lax.rsqrt(jnp.mean(xf * xf, axis=-1, keepdims=True) + EPS)
    return xf.astype(x.dtype) * g


def t5_bucket(dist):
    dist = jnp.maximum(dist, 0)
    max_exact = N_BUCKETS // 2
    logd = jnp.log(jnp.maximum(dist, 1).astype(jnp.float32) / max_exact)
    large = max_exact + (logd / math.log(MAX_DISTANCE / max_exact) * (N_BUCKETS - max_exact)).astype(jnp.int32)
    return jnp.where(dist < max_exact, dist, jnp.minimum(large, N_BUCKETS - 1))


def masked_softmax(logits, mask):
    lg = jnp.where(mask, logits.astype(jnp.float32), -jnp.inf)
    m = jnp.max(lg, axis=-1, keepdims=True)
    p = jnp.exp(lg - jnp.where(jnp.isfinite(m), m, 0.0))
    return p / jnp.maximum(jnp.sum(p, axis=-1, keepdims=True), 1e-30)


def query_blocks(tq):
    qb = Q_BLOCK if tq % Q_BLOCK == 0 else tq
    return qb, tq // qb


def sweep(block_fn, n_blocks, batch, tq):
    out = lax.map(block_fn, jnp.arange(n_blocks))
    return jnp.moveaxis(out, 0, 1).reshape(batch, tq, -1)


def gather_pages(pool, page_table):
    rows = pool[page_table]
    return rows.reshape((page_table.shape[0], -1) + pool.shape[2:])


def project(x, g_attn, w_in):
    b, t, _ = x.shape
    z = rmsnorm(x, g_attn) @ w_in
    q_n, k_c, v_c, k_s, v_s, k_w, v_w, gate, q_b, k_b, v_b = jnp.split(z, np.cumsum(IN_SIZES)[:-1].tolist(), axis=-1)
    heads = lambda a: a.reshape(b, t, -1, HEAD_DIM)
    gate = jax.nn.sigmoid(gate.astype(jnp.float32)).astype(x.dtype).reshape(b, t, NSA_HEADS, 3)
    return (heads(q_n), heads(k_c), heads(v_c), heads(k_s), heads(v_s), heads(k_w), heads(v_w),
            gate, heads(q_b), heads(k_b), heads(v_b))


def compress(rows, w_cmp, pe_cmp):
    b, t, g, d = rows.shape
    r = CMP_BLOCK // CMP_STRIDE
    chunks = rows.reshape(b, t // CMP_STRIDE, CMP_STRIDE, g, d)
    parts = jnp.einsum('bclgd,mlde->mbcge', chunks, w_cmp.reshape(r, CMP_STRIDE, d, d))
    n_cmp = t // CMP_STRIDE - r + 1
    out = parts[0, :, :n_cmp]
    for m in range(1, r):
        out = out + parts[m, :, m:m + n_cmp]
    return out + jnp.einsum('ld,lde->e', pe_cmp, w_cmp)


def nsa_mixer(q, gate, k_c, v_c, k_s, v_s, kw_src, vw_src, pos0, w_cmp_k, w_cmp_v, pe_cmp, rel_table):
    b, tq, _, d = q.shape
    g, h = NSA_KV_GROUPS, NSA_HPG
    tk = k_c.shape[1]
    t_pad = -(-tk // SEL_BLOCK) * SEL_BLOCK
    pad = ((0, 0), (0, t_pad - tk), (0, 0), (0, 0))
    kc = compress(jnp.pad(k_c, pad), w_cmp_k, pe_cmp)
    vc = compress(jnp.pad(v_c, pad), w_cmp_v, pe_cmp)
    n_cmp, n_sel = kc.shape[1], t_pad // SEL_BLOCK
    cmp_start = jnp.arange(n_cmp) * CMP_STRIDE
    cmp_end = cmp_start + CMP_BLOCK - 1
    sel_start = jnp.arange(n_sel) * SEL_BLOCK
    overlap = jnp.clip(jnp.minimum(cmp_end[:, None] + 1, sel_start[None] + SEL_BLOCK)
                       - jnp.maximum(cmp_start[:, None], sel_start[None]), 0, None).astype(jnp.float32) / CMP_BLOCK
    ks = jnp.pad(k_s, pad).reshape(b, n_sel, SEL_BLOCK, g, d).transpose(0, 3, 1, 2, 4)
    vs = jnp.pad(v_s, pad).reshape(b, n_sel, SEL_BLOCK, g, d).transpose(0, 3, 1, 2, 4)
    k_top = min(N_SEL, n_sel)
    n_key = k_top * SEL_BLOCK
    bi = jnp.arange(b)[:, None, None, None]
    gi = jnp.arange(g)[None, :, None, None]
    tbl_g = rel_table.reshape(N_BUCKETS, g, h)
    qg = q.reshape(b, tq, g, h, d) * (d ** -0.5)
    qb, nb = query_blocks(tq)

    def block(i):
        qi = lax.dynamic_slice_in_dim(qg, i * qb, qb, axis=1)
        t = pos0 + i * qb + jnp.arange(qb)
        lc = jnp.einsum('bqghd,bcgd->bghqc', qi, kc)
        bias_c = rel_table[t5_bucket(t[:, None] - cmp_end[None])].reshape(qb, n_cmp, g, h).transpose(2, 3, 0, 1)
        pc = masked_softmax(lc + bias_c, cmp_end[None] <= t[:, None])
        o_c = jnp.einsum('bghqc,bcgd->bqghd', pc.astype(vc.dtype), vc)
        p_sel = jnp.einsum('bgqc,cs->bgqs', pc.sum(axis=2), overlap)
        blk = jnp.arange(n_sel)[None]
        cur = (t // SEL_BLOCK)[:, None]
        forced = (blk == 0) | (blk == cur) | (blk == cur - 1)
        score = jnp.where(forced, jnp.inf, jnp.where(blk * SEL_BLOCK <= t[:, None], p_sel, -jnp.inf))
        _, idx = lax.top_k(score, k_top)
        k_sel = ks[bi, gi, idx].reshape(b, g, qb, n_key, d)
        v_sel = vs[bi, gi, idx].reshape(b, g, qb, n_key, d)
        s_pos = (idx[..., None] * SEL_BLOCK + jnp.arange(SEL_BLOCK)).reshape(b, g, qb, n_key)
        dist_s = t[:, None] - s_pos
        bias_s = tbl_g[t5_bucket(dist_s), gi].transpose(0, 1, 4, 2, 3)
        ls = jnp.einsum('bqghd,bgqnd->bghqn', qi, k_sel) + bias_s
        ps = masked_softmax(ls, (dist_s >= 0)[:, :, None])
        o_s = jnp.einsum('bghqn,bgqnd->bqghd', ps.astype(v_sel.dtype), v_sel)
        kw = lax.dynamic_slice_in_dim(kw_src, i * qb, qb + WINDOW, axis=1)
        vw = lax.dynamic_slice_in_dim(vw_src, i * qb, qb + WINDOW, axis=1)
        w_pos = pos0 - WINDOW + i * qb + jnp.arange(qb + WINDOW)
        dist_w = t[:, None] - w_pos[None]
        lw = jnp.einsum('bqghd,bkgd->bghqk', qi, kw)
        bias_w = rel_table[t5_bucket(dist_w)].reshape(qb, qb + WINDOW, g, h).transpose(2, 3, 0, 1)
        pw = masked_softmax(lw + bias_w, (dist_w >= 0) & (dist_w < WINDOW) & (w_pos[None] >= 0))
        o_w = jnp.einsum('bghqk,bkgd->bqghd', pw.astype(vw.dtype), vw)
        gt = lax.dynamic_slice_in_dim(gate, i * qb, qb, axis=1).reshape(b, qb, g, h, 3)
        out = gt[..., 0:1] * o_c + gt[..., 1:2] * o_s + gt[..., 2:3] * o_w
        return out.reshape(b, qb, g * h * d)

    return sweep(block, nb, b, tq)


def stick_breaking(q, k, v, pos0):
    b, tq, _, d = q.shape
    k_pos = jnp.arange(k.shape[1])
    qs = q * (d ** -0.5)
    qb, nb = query_blocks(tq)

    def block(i):
        qi = lax.dynamic_slice_in_dim(qs, i * qb, qb, axis=1)
        t = pos0 + i * qb + jnp.arange(qb)
        z = jnp.einsum('bqhd,bkhd->bhqk', qi, k).astype(jnp.float32)
        m = k_pos[None] < t[:, None]
        sp = jnp.where(m, jax.nn.softplus(z), 0.0)
        after = lax.cumsum(sp, axis=3, reverse=True) - sp
        log_a = jnp.where(m, jax.nn.log_sigmoid(z) - after, -jnp.inf)
        o = jnp.einsum('bhqk,bkhd->bqhd', jnp.exp(log_a).astype(v.dtype), v)
        return o.reshape(b, qb, -1)

    return sweep(block, nb, b, tq)


def mem_kv(mem, g_mem, w_mk, w_mv):
    b, m, _ = mem.shape
    mn = rmsnorm(mem, g_mem)
    return ((mn @ w_mk).reshape(b, m, MEM_HEADS, HEAD_DIM), (mn @ w_mv).reshape(b, m, MEM_HEADS, HEAD_DIM))


def layer_tail(x, o_nsa, o_sb, mem_k, mem_v, w_out, g_cross, w_mq, w_mo, g_ffn, w_gate, w_up, w_down):
    b, t, _ = x.shape
    h = x + jnp.concatenate([o_nsa, o_sb], axis=-1) @ w_out
    q = (rmsnorm(h, g_cross) @ w_mq).reshape(b, t, MEM_HEADS, HEAD_DIM) * (HEAD_DIM ** -0.5)
    p = jax.nn.softmax(jnp.einsum('bqhd,bmhd->bhqm', q, mem_k).astype(jnp.float32), axis=-1)
    o = jnp.einsum('bhqm,bmhd->bqhd', p.astype(mem_v.dtype), mem_v).reshape(b, t, MEM_WIDTH)
    h = h + o @ w_mo
    u = rmsnorm(h, g_ffn)
    return h + (jax.nn.silu(u @ w_gate) * (u @ w_up)) @ w_down


def setup_inputs(seed: int = 0) -> dict:
    keys = iter(jax.random.split(jax.random.key(seed), 40))

    def nrm(shape, scale=1.0):
        return scale * jax.random.normal(next(keys), shape, jnp.float32)

    def gain(shape):
        return 1.0 + nrm(shape, 0.05)

    n_pages = PAST_LEN // PAGE_SIZE
    n_used = DEC_BATCH * n_pages
    n_pool = n_used + n_used // 4
    win_buf = min(WINDOW, PAST_LEN)
    L, D, G = DEPTH, D_MODEL, NSA_KV_GROUPS
    page_table = jax.random.permutation(next(keys), n_pool)[:n_used].reshape(DEC_BATCH, n_pages).astype(jnp.int32)
    return {
        'x_prompt': nrm((BATCH, SEQ, D)),
        'x_sample': nrm((DEC_BATCH, DEC_SEQ, D)),
        'cache_cmp_k': nrm((L, n_pool, PAGE_SIZE, G, HEAD_DIM)),
        'cache_cmp_v': nrm((L, n_pool, PAGE_SIZE, G, HEAD_DIM)),
        'cache_slc_k': nrm((L, n_pool, PAGE_SIZE, G, HEAD_DIM)),
        'cache_slc_v': nrm((L, n_pool, PAGE_SIZE, G, HEAD_DIM)),
        'cache_sb_k': nrm((L, n_pool, PAGE_SIZE, SB_HEADS, HEAD_DIM)),
        'cache_sb_v': nrm((L, n_pool, PAGE_SIZE, SB_HEADS, HEAD_DIM)),
        'cache_win_k': nrm((L, DEC_BATCH, win_buf, G, HEAD_DIM)),
        'cache_win_v': nrm((L, DEC_BATCH, win_buf, G, HEAD_DIM)),
        'cache_mem_k': nrm((L, DEC_BATCH, MEM_TOKENS, MEM_HEADS, HEAD_DIM)),
        'cache_mem_v': nrm((L, DEC_BATCH, MEM_TOKENS, MEM_HEADS, HEAD_DIM)),
        'page_table': page_table,
        'mem_prompt': nrm((BATCH, MEM_TOKENS, D)),
        'rel_table': nrm((N_BUCKETS, NSA_HEADS), 0.5),
        'g_attn': gain((L, D)),
        'w_in': nrm((L, D, IN_WIDTH), D ** -0.5),
        'w_cmp_k': nrm((L, CMP_BLOCK, HEAD_DIM, HEAD_DIM), (CMP_BLOCK * HEAD_DIM) ** -0.5),
        'w_cmp_v': nrm((L, CMP_BLOCK, HEAD_DIM, HEAD_DIM), (CMP_BLOCK * HEAD_DIM) ** -0.5),
        'pe_cmp': nrm((L, CMP_BLOCK, HEAD_DIM), 0.1),
        'w_out': nrm((L, MIX_WIDTH, D), MIX_WIDTH ** -0.5),
        'g_mem': gain((L, D)),
        'w_mk': nrm((L, D, MEM_WIDTH), D ** -0.5),
        'w_mv': nrm((L, D, MEM_WIDTH), D ** -0.5),
        'g_cross': gain((L, D)),
        'w_mq': nrm((L, D, MEM_WIDTH), D ** -0.5),
        'w_mo': nrm((L, MEM_WIDTH, D), MEM_WIDTH ** -0.5),
        'g_ffn': gain((L, D)),
        'w_gate': nrm((L, D, D_FF), D ** -0.5),
        'w_up': nrm((L, D, D_FF), D ** -0.5),
        'w_down': nrm((L, D_FF, D), D_FF ** -0.5),
        'g_final': gain((D,)),
    }


def reference(x_prompt, x_sample, cache_cmp_k, cache_cmp_v, cache_slc_k, cache_slc_v, cache_sb_k, cache_sb_v,
              cache_win_k, cache_win_v, cache_mem_k, cache_mem_v, page_table, mem_prompt, rel_table, g_attn, w_in,
              w_cmp_k, w_cmp_v, pe_cmp, w_out, g_mem, w_mk, w_mv, g_cross, w_mq, w_mo, g_ffn, w_gate, w_up, w_down,
              g_final):
    past_len = page_table.shape[1] * cache_cmp_k.shape[2]
    win_buf = cache_win_k.shape[2]
    xp, xs = x_prompt, x_sample
    seq, n_dec = xp.shape[1], xs.shape[0]
    keep = min(WINDOW, seq)
    front = ((0, 0), (WINDOW, 0), (0, 0), (0, 0))
    prompt_states, sample_states = [], []
    for l in range(DEPTH):
        tail_w = (w_out[l], g_cross[l], w_mq[l], w_mo[l], g_ffn[l], w_gate[l], w_up[l], w_down[l])
        q_n, k_c, v_c, k_s, v_s, k_w, v_w, gate, q_b, k_b, v_b = project(xp, g_attn[l], w_in[l])
        o_nsa = nsa_mixer(q_n, gate, k_c, v_c, k_s, v_s, jnp.pad(k_w, front), jnp.pad(v_w, front), 0,
                          w_cmp_k[l], w_cmp_v[l], pe_cmp[l], rel_table)
        o_sb = stick_breaking(q_b, k_b, v_b, 0)
        m_k, m_v = mem_kv(mem_prompt, g_mem[l], w_mk[l], w_mv[l])
        xp = layer_tail(xp, o_nsa, o_sb, m_k, m_v, *tail_w)
        prompt_states.append((k_c, v_c, k_s, v_s, k_b, v_b, k_w[:, seq - keep:], v_w[:, seq - keep:], m_k, m_v))
        q_n, k_c, v_c, k_s, v_s, k_w, v_w, gate, q_b, k_b, v_b = project(xs, g_attn[l], w_in[l])

        def full(pool, new):
            return jnp.concatenate([gather_pages(pool[l], page_table), new], axis=1)

        pad_w = jnp.zeros((n_dec, WINDOW - win_buf) + k_w.shape[2:], k_w.dtype)
        win_k = jnp.concatenate([cache_win_k[l], k_w], axis=1)
        win_v = jnp.concatenate([cache_win_v[l], v_w], axis=1)
        o_nsa = nsa_mixer(q_n, gate, full(cache_cmp_k, k_c), full(cache_cmp_v, v_c),
                          full(cache_slc_k, k_s), full(cache_slc_v, v_s),
                          jnp.concatenate([pad_w, win_k], axis=1), jnp.concatenate([pad_w, win_v], axis=1),
                          past_len, w_cmp_k[l], w_cmp_v[l], pe_cmp[l], rel_table)
        o_sb = stick_breaking(q_b, full(cache_sb_k, k_b), full(cache_sb_v, v_b), past_len)
        xs = layer_tail(xs, o_nsa, o_sb, cache_mem_k[l], cache_mem_v[l], *tail_w)
        sample_states.append((k_c, v_c, k_s, v_s, k_b, v_b, win_k[:, -win_buf:], win_v[:, -win_buf:]))
    p_cmp_k, p_cmp_v, p_slc_k, p_slc_v, p_sb_k, p_sb_v, p_win_k, p_win_v, p_mem_k, p_mem_v = [
        jnp.stack(s) for s in zip(*prompt_states)]
    s_cmp_k, s_cmp_v, s_slc_k, s_slc_v, s_sb_k, s_sb_v, s_win_k, s_win_v = [
        jnp.stack(s) for s in zip(*sample_states)]
    y_prompt = rmsnorm(xp, g_final)
    y_sample = rmsnorm(xs, g_final)
    return (y_prompt, y_sample,
            p_cmp_k, p_cmp_v, p_slc_k, p_slc_v, p_sb_k, p_sb_v, p_win_k, p_win_v, p_mem_k, p_mem_v,
            s_cmp_k, s_cmp_v, s_slc_k, s_slc_v, s_sb_k, s_sb_v, s_win_k, s_win_v)
```

```python
import functools
import math

import numpy as np
import jax
import jax.numpy as jnp
from jax import lax
from jax.experimental import pallas as pl
from jax.experimental.pallas import tpu as pltpu

F32 = jnp.float32
BF16 = jnp.bfloat16

HEAD_DIM = 64
NSA_HEADS = 8
NSA_GROUPS = 2
HEADS_PER_GROUP = NSA_HEADS // NSA_GROUPS
SB_HEADS = 8
CMP_BLOCK = 32
CMP_STRIDE = 16
SEL_BLOCK = 64
N_SEL = 16
WINDOW = 512
N_BUCKETS = 32
MAX_DISTANCE = 128
MEM_HEADS = 4
EPS = 1e-6
QB = 128
KT = 128
LANES = 128
NEG = -1e30
M_INIT = -1e29
SB_DEAD = 110.0
VMEM_LIMIT = 56 * 1024 * 1024

_NT = (((1,), (1,)), ((), ()))
_TN = (((0,), (0,)), ((), ()))


def _dot(a, b):
    return jnp.dot(a, b, preferred_element_type=F32)


def _dot_nt(a, b):
    return lax.dot_general(a, b, _NT, preferred_element_type=F32)


def _bucket_np(dist):
    dist = np.maximum(np.asarray(dist, np.int64), 0)
    max_exact = N_BUCKETS // 2
    logd = np.log(np.maximum(dist, 1).astype(np.float32) / np.float32(max_exact))
    large = max_exact + (logd / np.float32(math.log(MAX_DISTANCE / max_exact))
                         * np.float32(N_BUCKETS - max_exact)).astype(np.int32)
    return np.where(dist < max_exact, dist, np.minimum(large, N_BUCKETS - 1)).astype(np.int32)


def _rms(x, g):
    xf = x * lax.rsqrt(jnp.mean(x * x, axis=-1, keepdims=True) + EPS)
    return xf * g


def _cparams(sem):
    return pltpu.CompilerParams(dimension_semantics=sem, vmem_limit_bytes=VMEM_LIMIT)


_NAT_F32 = (128,) * 6 + (512, 512)
_N_QPAD = NSA_HEADS * LANES


def _proj_kernel(x_ref, g_ref, wn_ref, wt_ref,
                 kc, vc, ks, vs, kw, vw, kb, vb,
                 qn, qb, ksb, kwb, kbb, vsT, vwT, vbT, gT):
    n = _rms(x_ref[0], g_ref[...]).astype(BF16)
    z = _dot(n, wn_ref[...])
    off = 0
    for ref, w in zip((kc, vc, ks, vs, kw, vw, kb, vb), _NAT_F32):
        ref[0] = z[:, off:off + w]
        off += w
    ksb[0] = z[:, 256:384].astype(BF16)
    kwb[0] = z[:, 512:640].astype(BF16)
    kbb[0] = z[:, 768:1280].astype(BF16)
    for h in range(NSA_HEADS):
        qn[0, h] = z[:, off + h * LANES:off + (h + 1) * LANES].astype(BF16)
    off += _N_QPAD
    for h in range(SB_HEADS):
        qb[0, h] = z[:, off + h * LANES:off + (h + 1) * LANES].astype(BF16)
    zt = _dot_nt(wt_ref[...], n)
    vsT[0] = zt[0:128].astype(BF16)
    vwT[0] = zt[128:256].astype(BF16)
    vbT[0] = zt[256:768].astype(BF16)
    gT[0] = jax.nn.sigmoid(zt[768:800]).reshape(NSA_GROUPS, 16, zt.shape[1])


def _proj_weights(w_in):
    sizes = (512, 128, 128, 128, 128, 128, 128, 24, 512, 512, 512)
    offs = np.concatenate([[0], np.cumsum(sizes)])
    col = lambda i: w_in[:, offs[i]:offs[i + 1]]
    q_n, k_c, v_c, k_s, v_s, k_w, v_w, gate, q_b, k_b, v_b = [col(i) for i in range(11)]
    d = w_in.shape[0]
    scale = HEAD_DIM ** -0.5
    zeros = jnp.zeros((d, HEAD_DIM), w_in.dtype)
    qn_cols, qb_cols = [], []
    for h in range(NSA_HEADS):
        wh = q_n[:, h * HEAD_DIM:(h + 1) * HEAD_DIM] * scale
        qn_cols += [wh, zeros] if h // HEADS_PER_GROUP == 0 else [zeros, wh]
    for h in range(SB_HEADS):
        wh = q_b[:, h * HEAD_DIM:(h + 1) * HEAD_DIM] * scale
        qb_cols += [wh, zeros] if h % 2 == 0 else [zeros, wh]
    w_nat = jnp.concatenate([k_c, v_c, k_s, v_s, k_w, v_w, k_b, v_b] + qn_cols + qb_cols, axis=1)
    gate_rows = []
    for g in range(NSA_GROUPS):
        for r in range(16):
            br, h = divmod(r, HEADS_PER_GROUP)
            if br < 3:
                c = (HEADS_PER_GROUP * g + h) * 3 + br
                gate_rows.append(gate[:, c])
            else:
                gate_rows.append(jnp.zeros((d,), w_in.dtype))
    w_t = jnp.concatenate([v_s.T, v_w.T, v_b.T, jnp.stack(gate_rows)], axis=0)
    return w_nat.astype(BF16), w_t.astype(BF16)


def _project(x, g, w_nat, w_t, tr):
    b, t, d = x.shape
    nt = t // tr
    row = lambda w: pl.BlockSpec((1, tr, w), lambda bi, ti: (bi, ti, 0))
    col = lambda w: pl.BlockSpec((1, w, tr), lambda bi, ti: (bi, 0, ti))
    heads = pl.BlockSpec((1, NSA_HEADS, tr, LANES), lambda bi, ti: (bi, 0, ti, 0))
    const = lambda a: pl.BlockSpec(a.shape, lambda bi, ti: (0,) * a.ndim)
    out_shape = ([jax.ShapeDtypeStruct((b, t, w), F32) for w in _NAT_F32]
                 + [jax.ShapeDtypeStruct((b, NSA_HEADS, t, LANES), BF16)] * 2
                 + [jax.ShapeDtypeStruct((b, t, w), BF16) for w in (128, 128, 512)]
                 + [jax.ShapeDtypeStruct((b, w, t), BF16) for w in (128, 128, 512)]
                 + [jax.ShapeDtypeStruct((b, NSA_GROUPS, 16, t), F32)])
    out_specs = ([row(w) for w in _NAT_F32] + [heads, heads] + [row(w) for w in (128, 128, 512)]
                 + [col(w) for w in (128, 128, 512)]
                 + [pl.BlockSpec((1, NSA_GROUPS, 16, tr), lambda bi, ti: (bi, 0, 0, ti))])
    return pl.pallas_call(
        _proj_kernel, grid=(b, nt),
        in_specs=[row(d), const(g), const(w_nat), const(w_t)],
        out_specs=out_specs, out_shape=out_shape,
        compiler_params=_cparams(("parallel", "parallel")), name="proj",
    )(x, g, w_nat, w_t)


def _normproj_kernel(x_ref, g_ref, w_ref, o_ref):
    n = _rms(x_ref[...], g_ref[...]).astype(BF16)
    o_ref[...] = _dot(n, w_ref[...])


def _normproj(x, g, w, tr):
    r, d = x.shape
    return pl.pallas_call(
        _normproj_kernel, grid=(r // tr,),
        in_specs=[pl.BlockSpec((tr, d), lambda i: (i, 0)), pl.BlockSpec(g.shape, lambda i: (0, 0)),
                  pl.BlockSpec(w.shape, lambda i: (0, 0))],
        out_specs=pl.BlockSpec((tr, w.shape[1]), lambda i: (i, 0)),
        out_shape=jax.ShapeDtypeStruct((r, w.shape[1]), F32),
        compiler_params=_cparams(("parallel",)), name="normproj",
    )(x, g, w)


def _compress_rows(src_ref, wd_ref, nch, shift_scr):
    acc0 = jnp.zeros((nch, LANES), F32)
    acc1 = jnp.zeros((nch, LANES), F32)
    for l in range(CMP_STRIDE):
        xl = src_ref[pl.ds(l, nch, stride=CMP_STRIDE), :].astype(BF16)
        acc0 = acc0 + _dot(xl, wd_ref[l])
        acc1 = acc1 + _dot(xl, wd_ref[CMP_STRIDE + l])
    shift_scr[pl.ds(0, nch), :] = acc1
    shift_scr[pl.ds(nch, 8), :] = jnp.zeros((8, LANES), F32)
    return acc0 + shift_scr[pl.ds(1, nch), :]


def _pe_const(pe_ref, wf_ref):
    return _dot(pe_ref[...], wf_ref[...])[0:1, :]


def _compress_kernel(k_ref, v_ref, wk_ref, wv_ref, pe_ref, wkf_ref, wvf_ref, kc_ref, vcT_ref, shift_scr):
    nch = kc_ref.shape[1]
    keep = lax.broadcasted_iota(jnp.int32, (nch, 1), 0) < nch - 1
    kc = _compress_rows(k_ref.at[0], wk_ref, nch, shift_scr) + _pe_const(pe_ref, wkf_ref)
    kc_ref[0] = jnp.where(keep, kc, 0.0).astype(BF16)
    vc = _compress_rows(v_ref.at[0], wv_ref, nch, shift_scr) + _pe_const(pe_ref, wvf_ref)
    vcT_ref[0] = jnp.where(keep, vc, 0.0).T.astype(BF16)


def _compress_weights(w_cmp, pe):
    z = jnp.zeros_like(w_cmp)
    wd = jnp.concatenate([jnp.concatenate([w_cmp, z], axis=2), jnp.concatenate([z, w_cmp], axis=2)], axis=1)
    wf = w_cmp.reshape(CMP_BLOCK * HEAD_DIM, HEAD_DIM)
    wf = jnp.concatenate([wf, wf], axis=1)
    pe8 = jnp.broadcast_to(pe.reshape(1, CMP_BLOCK * HEAD_DIM), (8, CMP_BLOCK * HEAD_DIM))
    return wd.astype(BF16), wf.astype(BF16), pe8.astype(BF16)


def _compress(k_c, v_c, wdk, wdv, pe8, wfk, wfv):
    b, t, _ = k_c.shape
    nch = t // CMP_STRIDE
    full = lambda a: pl.BlockSpec(a.shape, lambda bi: (0,) * a.ndim)
    return pl.pallas_call(
        _compress_kernel, grid=(b,),
        in_specs=[pl.BlockSpec((1, t, LANES), lambda bi: (bi, 0, 0))] * 2
        + [full(wdk), full(wdv), full(pe8), full(wfk), full(wfv)],
        out_specs=[pl.BlockSpec((1, nch, LANES), lambda bi: (bi, 0, 0)),
                   pl.BlockSpec((1, LANES, nch), lambda bi: (bi, 0, 0))],
        out_shape=[jax.ShapeDtypeStruct((b, nch, LANES), BF16), jax.ShapeDtypeStruct((b, LANES, nch), BF16)],
        scratch_shapes=[pltpu.VMEM((nch + 8, LANES), F32)],
        compiler_params=_cparams(("parallel",)), name="compress",
    )(k_c, v_c, wdk, wdv, pe8, wfk, wfv)


_HW = HEADS_PER_GROUP * QB


def _flash_update(s, vT, m_scr, l_scr, acc_scr):
    m_old = m_scr[...]
    m_new = jnp.maximum(m_old, jnp.max(s, axis=0, keepdims=True))
    alpha = jnp.exp(m_old - m_new)
    p = jnp.exp(s - m_new)
    l_scr[...] = alpha * l_scr[...] + jnp.sum(p, axis=0, keepdims=True)
    acc_scr[...] = alpha * acc_scr[...] + _dot(vT, p.astype(BF16))
    m_scr[...] = m_new


def _flash_init(m_scr, l_scr, acc_scr):
    m_scr[...] = jnp.full(m_scr.shape, M_INIT, F32)
    l_scr[...] = jnp.zeros(l_scr.shape, F32)
    acc_scr[...] = jnp.zeros(acc_scr.shape, F32)


def _flash_result(l_scr, acc_scr):
    return acc_scr[...] / jnp.maximum(l_scr[...], 1e-30)


def _tile4(x):
    return jnp.concatenate([x] * HEADS_PER_GROUP, axis=1)


def _nsa_kernel(q_ref, kc_ref, vcT_ref, ks_ref, vsT_ref, kw_ref, vwT_ref, gate_ref,
                bnear_ref, bcmp_ref, bfar_ref, o_ref,
                s_scr, pc_scr, sel_scr, m_scr, l_scr, acc_scr, oc_scr, os_scr, *, k_top):
    g = pl.program_id(1)
    i = pl.program_id(2)
    nch = kc_ref.shape[1]
    n_sel = sel_scr.shape[0]
    q = q_ref[0, 0].reshape(_HW, LANES)
    bfar = bfar_ref[0]
    grow = pl.multiple_of(g * HEAD_DIM, HEAD_DIM)
    q_loc = lax.broadcasted_iota(jnp.int32, (1, QB), 1)
    tq = i * QB + q_loc
    tq4 = _tile4(tq)
    k_loc = lax.broadcasted_iota(jnp.int32, (KT, 1), 0)

    s_scr[...] = _dot_nt(kc_ref[0], q) + bfar
    start = jnp.maximum(i * 8 - 16, 0)
    r0 = start - (i * 8 - 16)
    s_scr[pl.ds(pl.multiple_of(start, 8), 24), :] += bcmp_ref[0, pl.ds(pl.multiple_of(r0, 8), 24), :]
    c_idx = lax.broadcasted_iota(jnp.int32, (nch, 1), 0)
    valid = (c_idx * CMP_STRIDE + (CMP_BLOCK - 1)) <= tq4
    s = jnp.where(valid, s_scr[...], NEG)
    m = jnp.max(s, axis=0, keepdims=True)
    p = jnp.where(valid, jnp.exp(s - m), 0.0)
    pn = p / jnp.maximum(jnp.sum(p, axis=0, keepdims=True), 1e-30)
    oc_scr[...] = _dot(vcT_ref[0, pl.ds(grow, HEAD_DIM), :], pn.astype(BF16))
    pc_scr[pl.ds(0, 8), :] = jnp.zeros((8, QB), F32)
    pc_scr[pl.ds(8, nch), :] = (pn[:, 0:QB] + pn[:, QB:2 * QB]) + (pn[:, 2 * QB:3 * QB] + pn[:, 3 * QB:4 * QB])
    pc_scr[pl.ds(8 + nch, 8), :] = jnp.zeros((8, QB), F32)

    tap = lambda o: pc_scr[pl.ds(8 + o, n_sel, stride=4), :]
    p_sel = 0.5 * tap(-1) + tap(0) + tap(1) + tap(2) + 0.5 * tap(3)
    blk = lax.broadcasted_iota(jnp.int32, (n_sel, 1), 0)
    cur = tq // SEL_BLOCK
    forced = (blk == 0) | (blk == cur) | (blk == cur - 1)
    score = jnp.where(forced, 3e38, jnp.where(blk * SEL_BLOCK <= tq, p_sel, NEG))
    sel = jnp.zeros((n_sel, QB), F32)
    for _ in range(k_top):
        best = jnp.max(score, axis=0, keepdims=True)
        first = jnp.min(jnp.where(score == best, blk, n_sel), axis=0, keepdims=True)
        hit = blk == first
        sel = jnp.where(hit, 1.0, sel)
        score = jnp.where(hit, -3e38, score)
    sel_scr[...] = sel

    def sel_mask(j):
        lo = sel_scr[pl.ds(2 * j, 1), :]
        hi = sel_scr[pl.ds(2 * j + 1, 1), :]
        return jnp.where(k_loc < SEL_BLOCK, lo, hi) > 0.0

    def sel_tile(j, bias, extra_mask):
        kpos = pl.multiple_of(j * KT, KT)
        s = _dot_nt(ks_ref[0, pl.ds(kpos, KT), :], q) + bias
        msk = sel_mask(j)
        if extra_mask is not None:
            msk = msk & extra_mask
        s = jnp.where(_tile4(msk), s, NEG)
        _flash_update(s, vsT_ref[0, pl.ds(grow, HEAD_DIM), pl.ds(kpos, KT)], m_scr, l_scr, acc_scr)

    causal = k_loc <= q_loc
    _flash_init(m_scr, l_scr, acc_scr)

    def far_body(j, carry):
        sel_tile(j, bfar, None)
        return carry

    lax.fori_loop(0, jnp.maximum(i - 1, 0), far_body, 0)

    @pl.when(i >= 1)
    def _():
        sel_tile(i - 1, bnear_ref[0, 1], None)

    sel_tile(i, bnear_ref[0, 0], causal)
    os_scr[...] = _flash_result(l_scr, acc_scr)

    _flash_init(m_scr, l_scr, acc_scr)

    def win_tile(dt, bias, msk):
        j = i - dt
        kpos = pl.multiple_of(j * KT, KT)
        s = _dot_nt(kw_ref[0, pl.ds(kpos, KT), :], q) + bias
        if msk is not None:
            s = jnp.where(_tile4(msk), s, NEG)
        _flash_update(s, vwT_ref[0, pl.ds(grow, HEAD_DIM), pl.ds(kpos, KT)], m_scr, l_scr, acc_scr)

    n_win_tiles = WINDOW // KT
    for dt in range(n_win_tiles, 0, -1):
        bias = bnear_ref[0, 1] if dt == 1 else bfar
        msk = (q_loc < k_loc) if dt == n_win_tiles else None
        pl.when(i >= dt)(functools.partial(win_tile, dt, bias, msk))
    win_tile(0, bnear_ref[0, 0], causal)
    ow = _flash_result(l_scr, acc_scr)

    oc = oc_scr[...]
    osel = os_scr[...]
    gate = gate_ref[0, 0]
    for h in range(HEADS_PER_GROUP):
        cols = slice(h * QB, (h + 1) * QB)
        o_ref[0, pl.ds(h * HEAD_DIM, HEAD_DIM), :] = (
            gate[h:h + 1] * oc[:, cols] + gate[4 + h:5 + h] * osel[:, cols] + gate[8 + h:9 + h] * ow[:, cols])


def _nsa_bias_tables(rel_table):
    hpg = HEADS_PER_GROUP
    k = np.arange(KT)[:, None]
    qq = np.arange(QB)[None, :]
    dist = np.stack([qq - k, KT + qq - k])
    near = jnp.where(jnp.asarray(dist >= 0)[..., None], rel_table[_bucket_np(dist)], 0.0)
    near = near.reshape(2, KT, QB, NSA_GROUPS, hpg).transpose(3, 0, 1, 4, 2).reshape(NSA_GROUPS, 2, KT, hpg * QB)
    far_h = rel_table[N_BUCKETS - 1]
    far = jnp.repeat(far_h.reshape(NSA_GROUPS, hpg), QB, axis=1).reshape(NSA_GROUPS, 1, hpg * QB)
    r = np.arange(40)[:, None]
    dc = qq - CMP_STRIDE * r + (2 * QB - CMP_BLOCK + 1)
    cmp_delta = jnp.where(jnp.asarray(dc >= 0)[..., None], rel_table[_bucket_np(dc)] - far_h, 0.0)
    cmp_delta = cmp_delta.reshape(40, QB, NSA_GROUPS, hpg).transpose(2, 0, 3, 1).reshape(NSA_GROUPS, 40, hpg * QB)
    return near.astype(F32), cmp_delta.astype(F32), far.astype(F32)


def _nsa_prompt(qn, kc, vcT, ksb, vsT, kwb, vwT, gT, tables):
    b, _, t, _ = qn.shape
    nq = t // QB
    nch = kc.shape[1]
    n_sel = t // SEL_BLOCK
    k_top = min(N_SEL, n_sel)
    near, cmp_delta, far = tables
    q5 = qn.reshape(b, NSA_GROUPS, HEADS_PER_GROUP, t, LANES)
    per_b = lambda a: pl.BlockSpec((1,) + a.shape[1:], lambda bi, gi, qi: (bi,) + (0,) * (a.ndim - 1))
    per_g = lambda a: pl.BlockSpec((1,) + a.shape[1:], lambda bi, gi, qi: (gi,) + (0,) * (a.ndim - 1))
    return pl.pallas_call(
        functools.partial(_nsa_kernel, k_top=k_top), grid=(b, NSA_GROUPS, nq),
        in_specs=[pl.BlockSpec((1, 1, HEADS_PER_GROUP, QB, LANES), lambda bi, gi, qi: (bi, gi, 0, qi, 0)),
                  per_b(kc), per_b(vcT), per_b(ksb), per_b(vsT), per_b(kwb), per_b(vwT),
                  pl.BlockSpec((1, 1, 16, QB), lambda bi, gi, qi: (bi, gi, 0, qi)),
                  per_g(near), per_g(cmp_delta), per_g(far)],
        out_specs=pl.BlockSpec((1, HEADS_PER_GROUP * HEAD_DIM, QB), lambda bi, gi, qi: (bi, gi, qi)),
        out_shape=jax.ShapeDtypeStruct((b, NSA_HEADS * HEAD_DIM, t), F32),
        scratch_shapes=[pltpu.VMEM((nch, _HW), F32), pltpu.VMEM((nch + 16, QB), F32), pltpu.VMEM((n_sel, QB), F32),
                        pltpu.VMEM((1, _HW), F32), pltpu.VMEM((1, _HW), F32), pltpu.VMEM((HEAD_DIM, _HW), F32),
                        pltpu.VMEM((HEAD_DIM, _HW), F32), pltpu.VMEM((HEAD_DIM, _HW), F32)],
        compiler_params=_cparams(("parallel", "parallel", "arbitrary")), name="nsa_prompt",
    )(q5, kc, vcT, ksb, vsT, kwb, vwT, gT, near, cmp_delta, far)


_SBW = 2 * QB


def _sb_terms(z):
    lg = jnp.log1p(jnp.exp(-jnp.abs(z)))
    return jnp.maximum(z, 0.0) + lg, jnp.minimum(z, 0.0) - lg


def _suffix_sums(u, sp):
    hi = sp.astype(BF16)
    lo = (sp - hi.astype(F32)).astype(BF16)
    return _dot(u, hi) + _dot(u, lo)


def _sb_tile(k_tile, vT_tile, q, u, mask, carry_scr, acc_scr):
    z = _dot_nt(k_tile, q)
    sp, ls = _sb_terms(z)
    if mask is not None:
        sp = jnp.where(mask, sp, 0.0)
    total = carry_scr[...] + _suffix_sums(u, sp)
    a = jnp.exp(ls - total)
    if mask is not None:
        a = jnp.where(mask, a, 0.0)
    a = a.astype(BF16)
    for h in range(2):
        acc_scr[:, h * QB:(h + 1) * QB] += _dot(vT_tile[h * HEAD_DIM:(h + 1) * HEAD_DIM], a[:, h * QB:(h + 1) * QB])
    carry_scr[...] += jnp.sum(sp, axis=0, keepdims=True)


def _sb_kernel(q_ref, k_ref, vT_ref, o_ref, carry_scr, acc_scr):
    i = pl.program_id(2)
    q = q_ref[0, 0].reshape(_SBW, LANES)
    k_loc = lax.broadcasted_iota(jnp.int32, (KT, 1), 0)
    q_loc = lax.broadcasted_iota(jnp.int32, (1, QB), 1)
    u = (lax.broadcasted_iota(jnp.int32, (KT, KT), 1) > lax.broadcasted_iota(jnp.int32, (KT, KT), 0)).astype(BF16)
    strict = jnp.concatenate([k_loc < q_loc] * 2, axis=1)
    carry_scr[...] = jnp.zeros(carry_scr.shape, F32)
    acc_scr[...] = jnp.zeros(acc_scr.shape, F32)

    def tile(j, mask):
        kpos = pl.multiple_of(j * KT, KT)
        _sb_tile(k_ref[0, pl.ds(kpos, KT), :], vT_ref[0, :, pl.ds(kpos, KT)], q, u, mask, carry_scr, acc_scr)

    tile(i, strict)

    def cond(state):
        j, alive = state
        return (j >= 0) & alive

    def body(state):
        j, _ = state
        tile(j, None)
        return j - 1, jnp.min(carry_scr[...]) < SB_DEAD

    lax.while_loop(cond, body, (i - 1, jnp.min(carry_scr[...]) < SB_DEAD))
    o_ref[0] = jnp.concatenate([acc_scr[:, 0:QB], acc_scr[:, QB:2 * QB]], axis=0)


def _sb_prompt(qb, kbb, vbT):
    b, _, t, _ = qb.shape
    nq = t // QB
    npair = SB_HEADS // 2
    q5 = qb.reshape(b, npair, 2, t, LANES)
    return pl.pallas_call(
        _sb_kernel, grid=(b, npair, nq),
        in_specs=[pl.BlockSpec((1, 1, 2, QB, LANES), lambda bi, pi, qi: (bi, pi, 0, qi, 0)),
                  pl.BlockSpec((1, t, LANES), lambda bi, pi, qi: (bi, 0, pi)),
                  pl.BlockSpec((1, LANES, t), lambda bi, pi, qi: (bi, pi, 0))],
        out_specs=pl.BlockSpec((1, 2 * HEAD_DIM, QB), lambda bi, pi, qi: (bi, pi, qi)),
        out_shape=jax.ShapeDtypeStruct((b, SB_HEADS * HEAD_DIM, t), F32),
        scratch_shapes=[pltpu.VMEM((1, _SBW), F32), pltpu.VMEM((HEAD_DIM, _SBW), F32)],
        compiler_params=_cparams(("parallel", "parallel", "arbitrary")), name="sb_prompt",
    )(q5, kbb, vbT)


_MEMW = MEM_HEADS * HEAD_DIM


def _softmax_lanes(s):
    m = jnp.max(s, axis=-1, keepdims=True)
    p = jnp.exp(s - m)
    return p / jnp.sum(p, axis=-1, keepdims=True)


def _cross_rows(qx, mk, mv):
    lane_head = lax.broadcasted_iota(jnp.int32, (1, _MEMW), 1) // HEAD_DIM
    out = jnp.zeros(qx.shape, F32)
    for h in range(MEM_HEADS):
        qh = jnp.where(lane_head == h, qx, 0.0).astype(BF16)
        p = _softmax_lanes(_dot_nt(qh, mk))
        out = jnp.where(lane_head == h, _dot(p.astype(BF16), mv), out)
    return out


def _ffn_tail(h, oc, wmo, gf, wg, wu, wd, gfin):
    h = h + _dot(oc.astype(BF16), wmo)
    u = _rms(h, gf).astype(BF16)
    a = _dot(u, wg)
    f = (a * jax.nn.sigmoid(a)) * _dot(u, wu)
    h = h + _dot(f.astype(BF16), wd)
    return _rms(h, gfin)


def _tail_prompt_kernel(x_ref, on_ref, ob_ref, mkv_ref, wo_ref, gc_ref, wq_ref, wmo_ref, gf_ref, wg_ref, wu_ref,
                        wd_ref, gfin_ref, y_ref):
    o = jnp.concatenate([on_ref[0].T, ob_ref[0].T], axis=1)
    h = x_ref[0] + _dot(o.astype(BF16), wo_ref[...])
    qx = _dot(_rms(h, gc_ref[...]).astype(BF16), wq_ref[...])
    mkv = mkv_ref[0].astype(BF16)
    oc = _cross_rows(qx, mkv[:, 0:_MEMW], mkv[:, _MEMW:2 * _MEMW])
    y_ref[0] = _ffn_tail(h, oc, wmo_ref[...], gf_ref[...], wg_ref[...], wu_ref[...], wd_ref[...], gfin_ref[...])


def _resident(a):
    nd = a.ndim
    return pl.BlockSpec(a.shape, lambda *idx: (0,) * nd, pipeline_mode=pl.Buffered(1))


def _tail_prompt(x, onT, obT, mkv, tw, tr):
    b, t, d = x.shape
    return pl.pallas_call(
        _tail_prompt_kernel, grid=(b, t // tr),
        in_specs=[pl.BlockSpec((1, tr, d), lambda bi, ti: (bi, ti, 0)),
                  pl.BlockSpec((1, onT.shape[1], tr), lambda bi, ti: (bi, 0, ti)),
                  pl.BlockSpec((1, obT.shape[1], tr), lambda bi, ti: (bi, 0, ti)),
                  pl.BlockSpec((1,) + mkv.shape[1:], lambda bi, ti: (bi, 0, 0))] + [_resident(w) for w in tw],
        out_specs=pl.BlockSpec((1, tr, d), lambda bi, ti: (bi, ti, 0)),
        out_shape=jax.ShapeDtypeStruct((b, t, d), F32),
        compiler_params=_cparams(("parallel", "parallel")), name="tail_prompt",
    )(x, onT, obT, mkv, *tw)


def _tail_weights(w_out, g_cross, w_mq, w_mo, g_ffn, w_gate, w_up, w_down, g_final):
    row = lambda g: g.reshape(1, -1)
    return (w_out.astype(BF16), row(g_cross), (w_mq * HEAD_DIM ** -0.5).astype(BF16), w_mo.astype(BF16), row(g_ffn),
            w_gate.astype(BF16), w_up.astype(BF16), w_down.astype(BF16), row(g_final))


def _layer_weights(l, g_attn, w_in, w_cmp_k, w_cmp_v, pe_cmp, w_out, g_mem, w_mk, w_mv, g_cross, w_mq, w_mo, g_ffn,
                   w_gate, w_up, w_down, g_final):
    w_nat, w_t = _proj_weights(w_in[l])
    wdk, wfk, pe8 = _compress_weights(w_cmp_k[l], pe_cmp[l])
    wdv, wfv, _ = _compress_weights(w_cmp_v[l], pe_cmp[l])
    zeros = jnp.zeros((HEAD_DIM, w_out.shape[2]), w_out.dtype)
    rows = []
    for h in range(NSA_HEADS):
        wh = w_out[l, h * HEAD_DIM:(h + 1) * HEAD_DIM]
        rows += [wh, zeros] if h // HEADS_PER_GROUP == 0 else [zeros, wh]
    wo_padded = jnp.concatenate(rows + [w_out[l, NSA_HEADS * HEAD_DIM:]], axis=0).astype(BF16)
    return dict(g_attn=g_attn[l].reshape(1, -1), w_nat=w_nat, w_t=w_t, wo_padded=wo_padded, wdk=wdk, wdv=wdv, wfk=wfk, wfv=wfv, pe8=pe8,
                g_mem=g_mem[l].reshape(1, -1), w_mkv=jnp.concatenate([w_mk[l], w_mv[l]], axis=1).astype(BF16),
                tail=_tail_weights(w_out[l], g_cross[l], w_mq[l], w_mo[l], g_ffn[l], w_gate[l], w_up[l], w_down[l],
                                   g_final))


def _prompt_layer(xp, mem_prompt, rel_table, lw):
    b, t, d = xp.shape
    outs = _project(xp, lw["g_attn"], lw["w_nat"], lw["w_t"], tr=min(512, t))
    k_c, v_c, k_s, v_s, k_w, v_w, k_b, v_b, qn, qb, ksb, kwb, kbb, vsT, vwT, vbT, gT = outs
    kc, vcT = _compress(k_c, v_c, lw["wdk"], lw["wdv"], lw["pe8"], lw["wfk"], lw["wfv"])
    onT = _nsa_prompt(qn, kc, vcT, ksb, vsT, kwb, vwT, gT, _nsa_bias_tables(rel_table))
    obT = _sb_prompt(qb, kbb, vbT)
    m = mem_prompt.shape[1]
    mkv = _normproj(mem_prompt.reshape(b * m, d), lw["g_mem"], lw["w_mkv"], tr=min(256, b * m)).reshape(b, m, 2 * _MEMW)
    y = _tail_prompt(xp, onT, obT, mkv, lw["tail"], tr=min(256, t))
    keep = min(WINDOW, t)
    heads = lambda a, n: a.reshape(b, -1, n, HEAD_DIM)
    state = (heads(k_c, 2), heads(v_c, 2), heads(k_s, 2), heads(v_s, 2), heads(k_b, 8), heads(v_b, 8),
             heads(k_w[:, t - keep:], 2), heads(v_w[:, t - keep:], 2),
             heads(mkv[..., 0:_MEMW], MEM_HEADS), heads(mkv[..., _MEMW:], MEM_HEADS))
    return y, state


def _wait_all(copies):
    for c in copies:
        c.wait()


def _start_all(copies):
    for c in copies:
        c.start()


def _scmp_kernel(pt_ref, q_ref, bias_ref, ovl_ref, wdk_ref, wdv_ref, pe_ref, wfk_ref, wfv_ref, ck_hbm, cv_hbm,
                 oc_ref, psel_ref, kbuf, vbuf, sem, shift_scr, *, n_pages, page_size, past_len, page_base):
    b = pl.program_id(0)
    nb = pl.num_programs(0)
    slot = b % 2
    nch = past_len // CMP_STRIDE

    def page_copies(seq, sl):
        out = []
        for j in range(n_pages):
            page = pt_ref[seq * n_pages + j] + page_base
            rows = pl.ds(j * page_size, page_size)
            out.append(pltpu.make_async_copy(ck_hbm.at[page], kbuf.at[sl, rows], sem.at[0, sl]))
            out.append(pltpu.make_async_copy(cv_hbm.at[page], vbuf.at[sl, rows], sem.at[1, sl]))
        return out

    @pl.when(b == 0)
    def _():
        _start_all(page_copies(0, 0))

    @pl.when(b + 1 < nb)
    def _():
        _start_all(page_copies(b + 1, 1 - slot))

    _wait_all(page_copies(b, slot))

    keep = lax.broadcasted_iota(jnp.int32, (nch, 1), 0) < nch - 1
    kc = _compress_rows(kbuf.at[slot], wdk_ref, nch, shift_scr) + _pe_const(pe_ref, wfk_ref)
    kc = jnp.where(keep, kc, 0.0).astype(BF16)
    vc = _compress_rows(vbuf.at[slot], wdv_ref, nch, shift_scr) + _pe_const(pe_ref, wfv_ref)
    vc = jnp.where(keep, vc, 0.0).astype(BF16)

    s = _dot_nt(q_ref[0], kc) + bias_ref[...]
    c_idx = lax.broadcasted_iota(jnp.int32, (1, nch), 1)
    valid = c_idx * CMP_STRIDE + (CMP_BLOCK - 1) <= past_len
    s = jnp.where(valid, s, NEG)
    m = jnp.max(s, axis=1, keepdims=True)
    p = jnp.where(valid, jnp.exp(s - m), 0.0)
    pn = p / jnp.maximum(jnp.sum(p, axis=1, keepdims=True), 1e-30)
    oc_ref[0] = _dot(pn.astype(BF16), vc)
    row = lax.broadcasted_iota(jnp.int32, (NSA_HEADS, 1), 0)
    g0 = jnp.sum(pn[0:HEADS_PER_GROUP], axis=0, keepdims=True)
    g1 = jnp.sum(pn[HEADS_PER_GROUP:NSA_HEADS], axis=0, keepdims=True)
    pcs = jnp.where(row == 0, g0, jnp.where(row == 1, g1, 0.0))
    hi = pcs.astype(BF16)
    r1 = pcs - hi.astype(F32)
    mid = r1.astype(BF16)
    lo = (r1 - mid.astype(F32)).astype(BF16)
    ovl = ovl_ref[...]
    psel_ref[0] = _dot(hi, ovl) + _dot(mid, ovl) + _dot(lo, ovl)


def _sample_cmp(pt_flat, q8, bias_c, ovl, lw, pool_k, pool_v, *, n_pages, page_size, page_base):
    db = q8.shape[0]
    past_len = n_pages * page_size
    nch = past_len // CMP_STRIDE
    nsp = ovl.shape[1]
    full = lambda a: pl.BlockSpec(a.shape, lambda bi, pt: (0,) * a.ndim)
    anyspec = pl.BlockSpec(memory_space=pl.ANY)
    kern = functools.partial(_scmp_kernel, n_pages=n_pages, page_size=page_size, past_len=past_len,
                             page_base=page_base)
    return pl.pallas_call(
        kern,
        grid_spec=pltpu.PrefetchScalarGridSpec(
            num_scalar_prefetch=1, grid=(db,),
            in_specs=[pl.BlockSpec((1, NSA_HEADS, LANES), lambda bi, pt: (bi, 0, 0)), full(bias_c), full(ovl),
                      full(lw["wdk"]), full(lw["wdv"]), full(lw["pe8"]), full(lw["wfk"]), full(lw["wfv"]),
                      anyspec, anyspec],
            out_specs=[pl.BlockSpec((1, NSA_HEADS, LANES), lambda bi, pt: (bi, 0, 0)),
                       pl.BlockSpec((1, NSA_HEADS, nsp), lambda bi, pt: (bi, 0, 0))],
            scratch_shapes=[pltpu.VMEM((2, past_len, LANES), F32), pltpu.VMEM((2, past_len, LANES), F32),
                            pltpu.SemaphoreType.DMA((2, 2)), pltpu.VMEM((nch + 8, LANES), F32)]),
        out_shape=[jax.ShapeDtypeStruct((db, NSA_HEADS, LANES), F32), jax.ShapeDtypeStruct((db, NSA_HEADS, nsp), F32)],
        compiler_params=_cparams(("arbitrary",)), name="sample_cmp",
    )(pt_flat, q8, bias_c, ovl, lw["wdk"], lw["wdv"], lw["pe8"], lw["wfk"], lw["wfv"], pool_k, pool_v)


def _sselect_kernel(p_ref, idx_ref, *, n_sel, k_top, cur):
    rows = p_ref.shape[0]
    blk = lax.broadcasted_iota(jnp.int32, (rows, 1), 0)
    forced = (blk == 0) | (blk == cur) | (blk == cur - 1)
    score = jnp.where(forced, 3e38, jnp.where(blk < n_sel, p_ref[...], -3e38))
    for k in range(k_top):
        best = jnp.max(score, axis=0, keepdims=True)
        first = jnp.min(jnp.where(score == best, blk, rows), axis=0, keepdims=True)
        idx_ref[pl.ds(k, 1), :] = first
        score = jnp.where(blk == first, -3e38, score)


def _sample_select(pselT, *, n_sel, k_top, cur):
    kern = functools.partial(_sselect_kernel, n_sel=n_sel, k_top=k_top, cur=cur)
    return pl.pallas_call(
        kern, out_shape=jax.ShapeDtypeStruct((k_top, pselT.shape[1]), jnp.int32), name="sample_select",
        compiler_params=pltpu.CompilerParams(vmem_limit_bytes=VMEM_LIMIT),
    )(pselT)


def _sb_page(q8b, k_page, v_page, ut, carry_scr, acc_scr):
    z = _dot_nt(q8b, k_page.astype(BF16))
    sp, ls = _sb_terms(z)
    hi = sp.astype(BF16)
    lo = (sp - hi.astype(F32)).astype(BF16)
    after = _dot(hi, ut) + _dot(lo, ut)
    a = jnp.exp(ls - carry_scr[...] - after)
    acc_scr[...] += _dot(a.astype(BF16), v_page.astype(BF16))
    carry_scr[...] += jnp.sum(sp, axis=1, keepdims=True)


def _bias_from_dist(dist, rel_ref, thresholds):
    reps = dist.shape[1] // LANES
    wide = lambda bkt: jnp.concatenate([rel_ref[bkt]] * reps, axis=1)
    bias = wide(0)
    for bkt in range(1, N_BUCKETS):
        bias = jnp.where(dist >= thresholds[bkt], wide(bkt), bias)
    return bias


def _sattn_kernel(pt_ref, sel_ref, qn_ref, qb_ref, gate_ref, oc_ref, new_ref, rel_ref, bwin_ref,
                  wk_ref, wv_ref, sk_hbm, sv_hbm, bk_hbm, bv_hbm,
                  on_ref, ob_ref, nwk_ref, nwv_ref,
                  ksel, vsel, kpre, vpre, kx, vx, sem_s, sem_b, sem_x, carry_scr, acc_scr,
                  *, n_pages, page_size, k_top, n_sel, thresholds, page_base):
    b = pl.program_id(0)
    nb = pl.num_programs(0)
    slot = b % 2
    past_len = n_pages * page_size
    blocks_per_page = page_size // SEL_BLOCK
    nkey = k_top * SEL_BLOCK

    def sel_block(seq, g, k):
        return sel_ref[(seq * NSA_GROUPS + g) * k_top + k]

    def sel_copies(seq, sl, g, k):
        s = sel_block(seq, g, k)
        page = pt_ref[seq * n_pages + s // blocks_per_page] + page_base
        rows = pl.ds((s % blocks_per_page) * SEL_BLOCK, SEL_BLOCK)
        dst = pl.ds(k * SEL_BLOCK, SEL_BLOCK)
        return (pltpu.make_async_copy(sk_hbm.at[page, rows], ksel.at[sl, g, dst], sem_s.at[0, sl]),
                pltpu.make_async_copy(sv_hbm.at[page, rows], vsel.at[sl, g, dst], sem_s.at[1, sl]))

    def pre_copies(seq, sl, jj):
        page = pt_ref[seq * n_pages + (n_pages - 1 - jj)] + page_base
        return (pltpu.make_async_copy(bk_hbm.at[page], kpre.at[sl, jj], sem_b.at[0, sl]),
                pltpu.make_async_copy(bv_hbm.at[page], vpre.at[sl, jj], sem_b.at[1, sl]))

    def prefetch(seq, sl):
        for g in range(NSA_GROUPS):
            for k in range(k_top):
                cached = sel_block(seq, g, k) < n_sel - 1

                @pl.when(cached)
                def _():
                    _start_all(sel_copies(seq, sl, g, k))

                @pl.when(jnp.logical_not(cached))
                def _():
                    dst = pl.ds(k * SEL_BLOCK, SEL_BLOCK)
                    ksel[sl, g, dst, :] = jnp.zeros((SEL_BLOCK, LANES), F32)
                    vsel[sl, g, dst, :] = jnp.zeros((SEL_BLOCK, LANES), F32)
        for jj in range(2):
            _start_all(pre_copies(seq, sl, jj))

    @pl.when(b == 0)
    def _():
        prefetch(0, 0)

    @pl.when(b + 1 < nb)
    def _():
        prefetch(b + 1, 1 - slot)

    for g in range(NSA_GROUPS):
        for k in range(k_top):
            @pl.when(sel_block(b, g, k) < n_sel - 1)
            def _():
                _wait_all(sel_copies(b, slot, g, k))
    for jj in range(2):
        _wait_all(pre_copies(b, slot, jj))

    q8 = qn_ref[0]
    new = new_ref[0]
    row = lax.broadcasted_iota(jnp.int32, (NSA_HEADS, 1), 0)
    bias0 = rel_ref[0][:, 0:1]

    def with_new_key(s, k_new, v_new, v_rows):
        s_new = jnp.sum(q8.astype(F32) * k_new.astype(BF16).astype(F32), axis=1, keepdims=True) + bias0
        m = jnp.maximum(jnp.max(s, axis=1, keepdims=True), s_new)
        p = jnp.exp(s - m)
        p_new = jnp.exp(s_new - m)
        l = jnp.sum(p, axis=1, keepdims=True) + p_new
        o = _dot(p.astype(BF16), v_rows) + p_new.astype(BF16).astype(F32) * v_new.astype(BF16).astype(F32)
        return o / l

    lane = lax.broadcasted_iota(jnp.int32, (1, nkey), 1)
    o_sel = jnp.zeros((NSA_HEADS, LANES), F32)
    for g in range(NSA_GROUPS):
        first_pos = jnp.zeros((1, nkey), jnp.int32)
        for k in range(k_top):
            first_pos = jnp.where(lane // SEL_BLOCK == k, sel_block(b, g, k) * SEL_BLOCK, first_pos)
        dist = past_len - (first_pos + lane % SEL_BLOCK)
        s = _dot_nt(q8, ksel[slot, g].astype(BF16)) + _bias_from_dist(jnp.maximum(dist, 0), rel_ref, thresholds)
        s = jnp.where(dist > 0, s, NEG)
        o_g = with_new_key(s, new[0:1], new[1:2], vsel[slot, g].astype(BF16))
        o_sel = jnp.where(row // HEADS_PER_GROUP == g, o_g, o_sel)

    wk = wk_ref[0]
    wv = wv_ref[0]
    nwin = wk.shape[0]
    w_lane = lax.broadcasted_iota(jnp.int32, (1, nwin), 1)
    s = _dot_nt(q8, wk.astype(BF16)) + bwin_ref[...]
    s = jnp.where((w_lane >= nwin - (WINDOW - 1)), s, NEG)
    o_win = with_new_key(s, new[2:3], new[3:4], wv.astype(BF16))

    gate = gate_ref[0]
    on_ref[0] = (gate[:, 0:LANES] * oc_ref[0] + gate[:, LANES:2 * LANES] * o_sel
                 + gate[:, 2 * LANES:3 * LANES] * o_win)

    srow = lax.broadcasted_iota(jnp.int32, (8, 1), 0)
    for src, dst, new_row in ((wk_ref, nwk_ref, new[2:3]), (wv_ref, nwv_ref, new[3:4])):
        dst[0, pl.ds(0, nwin - 8), :] = src[0, pl.ds(1, nwin - 8), :]
        last = pltpu.roll(src[0, pl.ds(nwin - 8, 8), :], 7, 0)
        dst[0, pl.ds(nwin - 8, 8), :] = jnp.where(srow == 7, new_row, last)

    q8b = qb_ref[0]
    ut = (lax.broadcasted_iota(jnp.int32, (page_size, page_size), 0)
          > lax.broadcasted_iota(jnp.int32, (page_size, page_size), 1)).astype(BF16)
    carry_scr[...] = jnp.zeros(carry_scr.shape, F32)
    acc_scr[...] = jnp.zeros(acc_scr.shape, F32)
    _sb_page(q8b, kpre[slot, 0], vpre[slot, 0], ut, carry_scr, acc_scr)

    @pl.when(jnp.min(carry_scr[...]) < SB_DEAD)
    def _():
        _sb_page(q8b, kpre[slot, 1], vpre[slot, 1], ut, carry_scr, acc_scr)

    def more(state):
        j, alive = state
        return (j >= 0) & alive

    def older_page(state):
        j, _ = state
        page = pt_ref[b * n_pages + j] + page_base
        copies = (pltpu.make_async_copy(bk_hbm.at[page], kx, sem_x.at[0]),
                  pltpu.make_async_copy(bv_hbm.at[page], vx, sem_x.at[1]))
        _start_all(copies)
        _wait_all(copies)
        _sb_page(q8b, kx[...], vx[...], ut, carry_scr, acc_scr)
        return j - 1, jnp.min(carry_scr[...]) < SB_DEAD

    lax.while_loop(more, older_page, (n_pages - 3, jnp.min(carry_scr[...]) < SB_DEAD))
    lane_head = lax.broadcasted_iota(jnp.int32, (1, SB_HEADS * HEAD_DIM), 1) // HEAD_DIM
    ob_ref[0] = jnp.sum(jnp.where(lane_head == row, acc_scr[...], 0.0), axis=0, keepdims=True)


def _sample_attn(pt_flat, sel_flat, q8, q8b, gate_s, oc8, new_rows, rel_b, bwin, win_k, win_v, slc_k, slc_v, sb_k, sb_v,
                 *, n_pages, page_size, k_top, n_sel, thresholds, page_base):
    db = q8.shape[0]
    nwin = win_k.shape[1]
    sbw = SB_HEADS * HEAD_DIM
    per_seq = lambda a: pl.BlockSpec((1,) + a.shape[1:], lambda bi, pt, sl: (bi,) + (0,) * (a.ndim - 1))
    full = lambda a: pl.BlockSpec(a.shape, lambda bi, pt, sl: (0,) * a.ndim)
    anyspec = pl.BlockSpec(memory_space=pl.ANY)
    kern = functools.partial(_sattn_kernel, n_pages=n_pages, page_size=page_size, k_top=k_top, n_sel=n_sel,
                             thresholds=thresholds, page_base=page_base)
    out_shape = [jax.ShapeDtypeStruct((db, NSA_HEADS, LANES), F32), jax.ShapeDtypeStruct((db, 1, sbw), F32),
                 jax.ShapeDtypeStruct(win_k.shape, F32), jax.ShapeDtypeStruct(win_v.shape, F32)]
    return pl.pallas_call(
        kern,
        grid_spec=pltpu.PrefetchScalarGridSpec(
            num_scalar_prefetch=2, grid=(db,),
            in_specs=[per_seq(q8), per_seq(q8b), per_seq(gate_s), per_seq(oc8), per_seq(new_rows), full(rel_b),
                      full(bwin), per_seq(win_k), per_seq(win_v), anyspec, anyspec, anyspec, anyspec],
            out_specs=[per_seq(o) for o in out_shape],
            scratch_shapes=[pltpu.VMEM((2, NSA_GROUPS, k_top * SEL_BLOCK, LANES), F32),
                            pltpu.VMEM((2, NSA_GROUPS, k_top * SEL_BLOCK, LANES), F32),
                            pltpu.VMEM((2, 2, page_size, sbw), F32), pltpu.VMEM((2, 2, page_size, sbw), F32),
                            pltpu.VMEM((page_size, sbw), F32), pltpu.VMEM((page_size, sbw), F32),
                            pltpu.SemaphoreType.DMA((2, 2)), pltpu.SemaphoreType.DMA((2, 2)),
                            pltpu.SemaphoreType.DMA((2,)),
                            pltpu.VMEM((NSA_HEADS, 1), F32), pltpu.VMEM((NSA_HEADS, sbw), F32)]),
        out_shape=out_shape,
        compiler_params=_cparams(("arbitrary",)), name="sample_attn",
    )(pt_flat, sel_flat, q8, q8b, gate_s, oc8, new_rows, rel_b, bwin, win_k, win_v, slc_k, slc_v, sb_k, sb_v)


def _tail_a_kernel(x_ref, on_ref, ob_ref, wo_ref, gc_ref, wq_ref, h_ref, qx_ref):
    o = jnp.concatenate([on_ref[...], ob_ref[...]], axis=1)
    h = x_ref[...] + _dot(o.astype(BF16), wo_ref[...])
    h_ref[...] = h
    qx_ref[...] = _dot(_rms(h, gc_ref[...]).astype(BF16), wq_ref[...])


def _cross_s_kernel(qx_ref, mk_ref, mv_ref, o_ref):
    row = lax.broadcasted_iota(jnp.int32, (8, 1), 0)
    lane_head = lax.broadcasted_iota(jnp.int32, (1, _MEMW), 1) // HEAD_DIM
    own = lane_head == row
    qx = qx_ref[...]
    for r in range(8):
        q8 = jnp.where(own, qx[r:r + 1], 0.0).astype(BF16)
        p = _softmax_lanes(_dot_nt(q8, mk_ref[r].astype(BF16)))
        o8 = _dot(p.astype(BF16), mv_ref[r].astype(BF16))
        o_ref[pl.ds(r, 1), :] = jnp.sum(jnp.where(own, o8, 0.0), axis=0, keepdims=True)


def _tail_b_kernel(h_ref, oc_ref, wmo_ref, gf_ref, wg_ref, wu_ref, wd_ref, gfin_ref, y_ref):
    y_ref[...] = _ffn_tail(h_ref[...], oc_ref[...], wmo_ref[...], gf_ref[...], wg_ref[...], wu_ref[...], wd_ref[...],
                           gfin_ref[...])


def _tail_sample(x, on, ob, mem_k, mem_v, wo, tw):
    _, gc, wq, wmo, gf, wg, wu, wd, gfin = tw
    r, d = x.shape
    cp = pltpu.CompilerParams(vmem_limit_bytes=VMEM_LIMIT)
    h, qx = pl.pallas_call(
        _tail_a_kernel, out_shape=[jax.ShapeDtypeStruct((r, d), F32), jax.ShapeDtypeStruct((r, _MEMW), F32)],
        compiler_params=cp, name="tail_a")(x, on, ob, wo, gc, wq)
    m = mem_k.shape[1]
    oc = pl.pallas_call(
        _cross_s_kernel, grid=(r // 8,),
        in_specs=[pl.BlockSpec((8, _MEMW), lambda i: (i, 0)), pl.BlockSpec((8, m, _MEMW), lambda i: (i, 0, 0)),
                  pl.BlockSpec((8, m, _MEMW), lambda i: (i, 0, 0))],
        out_specs=pl.BlockSpec((8, _MEMW), lambda i: (i, 0)),
        out_shape=jax.ShapeDtypeStruct((r, _MEMW), F32),
        compiler_params=_cparams(("parallel",)), name="cross_sample")(qx, mem_k, mem_v)
    return pl.pallas_call(
        _tail_b_kernel, out_shape=jax.ShapeDtypeStruct((r, d), F32), compiler_params=cp, name="tail_b",
    )(h, oc, wmo, gf, wg, wu, wd, gfin)


def _sample_layer(l, xs, caches, page_table, rel_table, lw):
    cmp_k, cmp_v, slc_k, slc_v, sb_k, sb_v, win_k, win_v, mem_k, mem_v = caches
    db, _, d = xs.shape
    n_pages = page_table.shape[1]
    n_pool, page_size = cmp_k.shape[1], cmp_k.shape[2]
    past_len = n_pages * page_size
    n_sel = past_len // SEL_BLOCK + 1
    k_top = min(N_SEL, n_sel)
    nch = past_len // CMP_STRIDE
    pool = lambda a, w: a.reshape(-1, page_size, w)
    page_base = l * n_pool

    outs = _project(xs.reshape(1, db, d), lw["g_attn"], lw["w_nat"], lw["w_t"], tr=db)
    k_c, v_c, k_s, v_s, k_w, v_w, k_b, v_b, qn, qb = outs[:10]
    gT = outs[16]
    q8 = jnp.transpose(qn[0], (1, 0, 2))
    pair_onehot = jnp.asarray(np.arange(SB_HEADS)[:, None] // 2 == np.arange(SB_HEADS // 2)[None, :], BF16)
    q8b = (jnp.transpose(qb[0], (1, 0, 2))[:, :, None, :] * pair_onehot[None, :, :, None]).reshape(db, SB_HEADS, -1)
    gate_s = jnp.transpose(gT[0].reshape(NSA_GROUPS, 4, HEADS_PER_GROUP, db), (3, 0, 2, 1)).reshape(db, NSA_HEADS, 4)
    gate_s = jnp.repeat(gate_s[:, :, 0:3], LANES, axis=2)
    new_rows = jnp.concatenate([k_s, v_s, k_w, v_w, jnp.zeros((4, db, LANES), F32)], axis=0).transpose(1, 0, 2)

    c = np.arange(nch)
    bias_c = rel_table[_bucket_np(past_len - (c * CMP_STRIDE + CMP_BLOCK - 1))].T
    nwin = win_k.shape[2]
    bwin = rel_table[_bucket_np(WINDOW - np.arange(nwin) + (nwin - WINDOW))].T
    rel_b = jnp.broadcast_to(rel_table[:, :, None], (N_BUCKETS, NSA_HEADS, LANES))
    bk = _bucket_np(np.arange(4 * MAX_DISTANCE))
    thresholds = tuple(int(np.argmax(bk >= v)) for v in range(N_BUCKETS))
    nsp = -(-n_sel // (2 * LANES)) * 2 * LANES
    cs, ce = c * CMP_STRIDE, c * CMP_STRIDE + CMP_BLOCK
    ss = np.arange(nsp) * SEL_BLOCK
    ovl = np.clip(np.minimum(ce[:, None], ss[None] + SEL_BLOCK) - np.maximum(cs[:, None], ss[None]), 0, None) / CMP_BLOCK
    ovl[nch - 1:] = 0.0
    ovl[:, n_sel:] = 0.0

    pt_flat = page_table.reshape(-1)
    oc8, psel = _sample_cmp(pt_flat, q8, bias_c, jnp.asarray(ovl, BF16), lw, pool(cmp_k, LANES), pool(cmp_v, LANES),
                            n_pages=n_pages, page_size=page_size, page_base=page_base)
    pselT = jnp.transpose(psel[:, 0:NSA_GROUPS, :], (2, 1, 0)).reshape(nsp, NSA_GROUPS * db)
    idx = _sample_select(pselT, n_sel=n_sel, k_top=k_top, cur=past_len // SEL_BLOCK)
    sel_flat = jnp.transpose(idx.reshape(k_top, NSA_GROUPS, db), (2, 1, 0)).reshape(-1)
    sbw = SB_HEADS * HEAD_DIM
    on8, ob, nwk, nwv = _sample_attn(
        pt_flat, sel_flat, q8, q8b, gate_s, oc8, new_rows, rel_b, bwin,
        win_k[l].reshape(db, nwin, LANES), win_v[l].reshape(db, nwin, LANES),
        pool(slc_k, LANES), pool(slc_v, LANES), pool(sb_k, sbw), pool(sb_v, sbw),
        n_pages=n_pages, page_size=page_size, k_top=k_top, n_sel=n_sel, thresholds=thresholds, page_base=page_base)
    m = mem_k.shape[2]
    y = _tail_sample(xs.reshape(db, d), on8.reshape(db, NSA_HEADS * LANES), ob.reshape(db, sbw),
                     mem_k[l].reshape(db, m, _MEMW), mem_v[l].reshape(db, m, _MEMW), lw["wo_padded"], lw["tail"])
    heads = lambda a, n: a.reshape(db, -1, n, HEAD_DIM)
    state = (heads(k_c, 2), heads(v_c, 2), heads(k_s, 2), heads(v_s, 2), heads(k_b, 8), heads(v_b, 8),
             heads(nwk, 2), heads(nwv, 2))
    return y.reshape(db, 1, d), state


def kernel(x_prompt, x_sample, cache_cmp_k, cache_cmp_v, cache_slc_k, cache_slc_v, cache_sb_k, cache_sb_v,
           cache_win_k, cache_win_v, cache_mem_k, cache_mem_v, page_table, mem_prompt, rel_table, g_attn, w_in,
           w_cmp_k, w_cmp_v, pe_cmp, w_out, g_mem, w_mk, w_mv, g_cross, w_mq, w_mo, g_ffn, w_gate, w_up, w_down,
           g_final):
    depth = g_attn.shape[0]
    assert depth == 1, "the layer tail applies the final norm, so exactly one trunk layer is supported"
    assert cache_win_k.shape[2] == WINDOW and page_table.shape[1] >= 3
    caches = (cache_cmp_k, cache_cmp_v, cache_slc_k, cache_slc_v, cache_sb_k, cache_sb_v, cache_win_k, cache_win_v,
              cache_mem_k, cache_mem_v)
    lw = _layer_weights(0, g_attn, w_in, w_cmp_k, w_cmp_v, pe_cmp, w_out, g_mem, w_mk, w_mv, g_cross, w_mq, w_mo,
                        g_ffn, w_gate, w_up, w_down, g_final)
    yp, p_state = _prompt_layer(x_prompt, mem_prompt, rel_table, lw)
    ys, s_state = _sample_layer(0, x_sample, caches, page_table, rel_table, lw)
    return (yp, ys) + tuple(s[None] for s in p_state) + tuple(s[None] for s in s_state)
```
